```python
import jax, jax.numpy as jnp
from jax import lax
import numpy as np

D_MODEL = 4096
BATCH = 32
SEQ = 256
DEPTH = 4
DEC_BATCH = 2
DEC_SEQ = 1024
PAST_LEN = 256

GRID_W = 64
N_MIXERS = 2
N_FOUR = (DEPTH + 1) // 2
N_RET = DEPTH // 2
FOURIER_GROUPS = 4
RET_HEADS = 16
RET_DK = D_MODEL // RET_HEADS
RET_DV = 2 * D_MODEL // RET_HEADS
CHUNK = 128
ROPE_BASE = 10000.0
ROPE_PAIRS_AXIS = RET_DK // 4
D_FF = 11008
CONV_W = 3
RMS_EPS = 1e-6
GN_EPS = 1e-5
MOD_CHUNKS = 6
RET_DECAY_LOGIT = np.log(2.0 ** (5.0 + np.arange(RET_HEADS)) - 1.0).astype(np.float32)

kernel_name = "hybrid_fourier_retention_dit_step"


def _rms_norm(x, w):
    x32 = x.astype(jnp.float32)
    y = x32 * lax.rsqrt(jnp.mean(x32 * x32, axis=-1, keepdims=True) + RMS_EPS)
    return (y * w.astype(jnp.float32)).astype(x.dtype)


def _grid_rope(length):
    t = jnp.arange(length)
    row = (t // GRID_W).astype(jnp.float32)
    col = (t % GRID_W).astype(jnp.float32)
    inv_freq = ROPE_BASE ** (-(jnp.arange(ROPE_PAIRS_AXIS, dtype=jnp.float32) / ROPE_PAIRS_AXIS))
    ang = jnp.concatenate([row[:, None] * inv_freq, col[:, None] * inv_freq], axis=-1)
    return jnp.cos(ang), jnp.sin(ang)


def _apply_rope(x, cos, sin):
    B, T, H, DK = x.shape
    xr = x.reshape(B, T, H, DK // 2, 2)
    x1, x2 = xr[..., 0], xr[..., 1]
    c = cos[None, :, None, :].astype(x.dtype)
    s = sin[None, :, None, :].astype(x.dtype)
    return jnp.stack([x1 * c - x2 * s, x1 * s + x2 * c], axis=-1).reshape(B, T, H, DK)


def _retention_scan(q, k, v, log_gamma, s0):
    B, H, T, DK = q.shape
    DV = v.shape[-1]
    n = T // CHUNK
    to_chunks = lambda a: a.reshape(B, H, n, CHUNK, a.shape[-1]).transpose(2, 0, 1, 3, 4)
    qc, kc, vc = to_chunks(q), to_chunks(k), to_chunks(v)
    pos = jnp.arange(CHUNK, dtype=jnp.float32)
    rel = pos[:, None] - pos[None, :]
    intra = jnp.where(rel >= 0, jnp.exp(log_gamma[:, None, None] * jnp.maximum(rel, 0.0)), 0.0)
    q_decay = jnp.exp(log_gamma[:, None] * (pos + 1.0)[None, :])[None, :, :, None]
    k_decay = jnp.exp(log_gamma[:, None] * (CHUNK - 1.0 - pos)[None, :])[None, :, :, None]
    chunk_decay = jnp.exp(log_gamma * CHUNK)[None, :, None, None]

    def step(s, blk):
        qb, kb, vb = blk
        scores = jnp.einsum("bhid,bhjd->bhij", qb, kb) * intra
        o = jnp.einsum("bhij,bhje->bhie", scores, vb) + jnp.einsum("bhid,bhde->bhie", qb * q_decay, s)
        s_new = s * chunk_decay + jnp.einsum("bhjd,bhje->bhde", kb * k_decay, vb)
        return s_new, o

    s_fin, o = lax.scan(step, s0, (qc, kc, vc))
    return o.transpose(1, 2, 0, 3, 4).reshape(B, H, T, DV), s_fin


def _retention_mixer(h, w_in, log_g, gn_w, w_out, s_init, rope):
    B, T, _ = h.shape
    hq = RET_HEADS * RET_DK
    hv = RET_HEADS * RET_DV
    proj = h @ w_in
    q = proj[..., :hq].reshape(B, T, RET_HEADS, RET_DK)
    k = proj[..., hq:2 * hq].reshape(B, T, RET_HEADS, RET_DK) * (RET_DK ** -0.5)
    v = proj[..., 2 * hq:2 * hq + hv].reshape(B, T, RET_HEADS, RET_DV)
    g = proj[..., 2 * hq + hv:]
    if rope is not None:
        q = _apply_rope(q, rope[0], rope[1])
        k = _apply_rope(k, rope[0], rope[1])
    to_bhtd = lambda a: a.transpose(0, 2, 1, 3).astype(jnp.float32)
    q, k, v = to_bhtd(q), to_bhtd(k), to_bhtd(v)
    flip = lambda a: jnp.flip(a, axis=2)
    o_f, s_f = _retention_scan(q, k, v, log_g[0], s_init[:, 0])
    o_b, s_b = _retention_scan(flip(q), flip(k), flip(v), log_g[1], s_init[:, 1])
    o = o_f + flip(o_b)
    mu = jnp.mean(o, axis=-1, keepdims=True)
    var = jnp.mean(jnp.square(o - mu), axis=-1, keepdims=True)
    o = (o - mu) * lax.rsqrt(var + GN_EPS)
    o = o.transpose(0, 2, 1, 3) * gn_w.astype(jnp.float32).reshape(RET_HEADS, RET_DV)
    o = o.reshape(B, T, hv).astype(h.dtype)
    y = (o * jax.nn.silu(g)) @ w_out
    return y, jnp.stack([s_f, s_b], axis=1)


def _fourier_mixer(h, w_fo, b_fo):
    B, T, D = h.shape
    hg = h.astype(jnp.float32).reshape(B, T, FOURIER_GROUPS, D // FOURIER_GROUPS)
    f = jnp.fft.fftn(hg, axes=(1, 3), norm="ortho").real
    return f.reshape(B, T, D).astype(h.dtype) @ w_fo + b_fo


def _dwconv3(u, w, b):
    pad = [(0, 0)] * (u.ndim - 2) + [(1, 1), (0, 0)]
    up = jnp.pad(u, pad)
    return w[0] * up[..., :-2, :] + w[1] * up[..., 1:-1, :] + w[2] * up[..., 2:, :] + b


def _conv_ffn(h, w_up, w_conv, b_conv, w_down, grid):
    B, T, _ = h.shape
    u = h @ w_up
    if grid:
        rows = T // GRID_W
        u = _dwconv3(u.reshape(B, rows, GRID_W, 2 * D_FF), w_conv, b_conv).reshape(B, T, 2 * D_FF)
    else:
        u = _dwconv3(u, w_conv, b_conv)
    a, v = u[..., :D_FF], u[..., D_FF:]
    return (jax.nn.silu(a) * v) @ w_down


def _trunk(x, cvec, s_init, rope, grid, w_mod, b_mod, norm_w, final_norm_w, w_fo, b_fo,
           w_in, w_out, ret_decay, ret_gn_w, w_up, w_conv, b_conv, w_down):
    B = x.shape[0]
    states = []
    for i in range(DEPTH):
        mod = jax.nn.silu(cvec) @ w_mod[i] + b_mod[i]
        sh1, sc1, g1, sh2, sc2, g2 = jnp.split(mod, MOD_CHUNKS, axis=-1)
        h = _rms_norm(x, norm_w[i, 0]) * (1.0 + sc1) + sh1
        j = i // N_MIXERS
        if i % N_MIXERS == 0:
            y = _fourier_mixer(h, w_fo[j], b_fo[j])
        else:
            if s_init is None:
                s0 = jnp.zeros((B, 2, RET_HEADS, RET_DK, RET_DV), jnp.float32)
            else:
                s0 = s_init[:, j].astype(jnp.float32)
            log_g = jax.nn.log_sigmoid(ret_decay[j].astype(jnp.float32))
            y, s = _retention_mixer(h, w_in[j], log_g, ret_gn_w[j], w_out[j], s0, rope)
            states.append(s)
        x = x + g1 * y
        h = _rms_norm(x, norm_w[i, 1]) * (1.0 + sc2) + sh2
        x = x + g2 * _conv_ffn(h, w_up[i], w_conv[i], b_conv[i], w_down[i], grid)
    return _rms_norm(x, final_norm_w), states


def setup_inputs(seed: int = 0) -> dict:
    key = jax.random.key(seed)
    ks = jax.random.split(key, 20)
    nrm = lambda k, shape, s: jax.random.normal(k, shape, jnp.float32) * s
    D = D_MODEL
    n_in = 2 * RET_HEADS * RET_DK + 2 * RET_HEADS * RET_DV
    decay = jnp.asarray(RET_DECAY_LOGIT)[None, None, :] + nrm(ks[11], (N_RET, 2, RET_HEADS), 0.1)
    return {
        "x_prompt": nrm(ks[0], (BATCH, SEQ, D), 1.0),
        "x_sample": nrm(ks[1], (DEC_BATCH, DEC_SEQ, D), 1.0),
        "state_ret": nrm(ks[2], (DEC_BATCH, N_RET, 2, RET_HEADS, RET_DK, RET_DV), 1.0),
        "c": nrm(ks[3], (DEC_BATCH, D), 1.0),
        "c_ctx": nrm(ks[4], (D,), 1.0),
        "w_mod": nrm(ks[5], (DEPTH, D, MOD_CHUNKS * D), 0.5 * D ** -0.5),
        "b_mod": nrm(ks[6], (DEPTH, MOD_CHUNKS * D), 0.01),
        "norm_w": 1.0 + nrm(ks[7], (DEPTH, 2, D), 0.02),
        "final_norm_w": 1.0 + nrm(ks[8], (D,), 0.02),
        "w_fo": nrm(ks[9], (N_FOUR, D, D), D ** -0.5),
        "b_fo": nrm(ks[10], (N_FOUR, D), 0.01),
        "w_in": nrm(ks[12], (N_RET, D, n_in), D ** -0.5),
        "w_out": nrm(ks[13], (N_RET, RET_HEADS * RET_DV, D), (RET_HEADS * RET_DV) ** -0.5),
        "ret_decay": decay,
        "ret_gn_w": 1.0 + nrm(ks[14], (N_RET, RET_HEADS * RET_DV), 0.02),
        "w_up": nrm(ks[15], (DEPTH, D, 2 * D_FF), D ** -0.5),
        "w_conv": nrm(ks[16], (DEPTH, CONV_W, 2 * D_FF), CONV_W ** -0.5),
        "b_conv": nrm(ks[17], (DEPTH, 2 * D_FF), 0.01),
        "w_down": nrm(ks[18], (DEPTH, D_FF, D), D_FF ** -0.5),
    }


def reference(x_prompt, x_sample, state_ret, c, c_ctx, w_mod, b_mod, norm_w, final_norm_w,
              w_fo, b_fo, w_in, w_out, ret_decay, ret_gn_w, w_up, w_conv, b_conv, w_down):
    y_prompt, ctx_states = _trunk(x_prompt, c_ctx, None, None, False, w_mod, b_mod, norm_w,
                                  final_norm_w, w_fo, b_fo, w_in, w_out, ret_decay, ret_gn_w,
                                  w_up, w_conv, b_conv, w_down)
    state_ret_new = jnp.stack(ctx_states, axis=1).astype(x_prompt.dtype)
    rope = _grid_rope(x_sample.shape[1])
    y_sample, _ = _trunk(x_sample, c[:, None, :], state_ret, rope, True, w_mod, b_mod, norm_w,
                         final_norm_w, w_fo, b_fo, w_in, w_out, ret_decay, ret_gn_w,
                         w_up, w_conv, b_conv, w_down)
    return (y_prompt, y_sample, state_ret_new)
```

```python
import functools

import numpy as np
import jax
import jax.numpy as jnp
from jax import lax
from jax.experimental import pallas as pl
from jax.experimental.pallas import tpu as pltpu

GRID_W = 64
FOURIER_GROUPS = 4
RET_HEADS = 16
CHUNK = 128
ROPE_BASE = 10000.0
RMS_EPS = 1e-6
GN_EPS = 1e-5
MOD_CHUNKS = 6
MOD_ROWS = 8
V7X_VMEM_LIMIT = 56 * 1024 * 1024

F32 = jnp.float32
BF16 = jnp.bfloat16


def _params(sem, vmem=V7X_VMEM_LIMIT):
    return pltpu.CompilerParams(dimension_semantics=sem, vmem_limit_bytes=vmem)


class _Tokens:
    def __init__(self, n_ctx, seq, n_dec, dec_seq):
        self.n_ctx, self.seq, self.n_dec, self.dec_seq = n_ctx, seq, n_dec, dec_seq
        self.m = n_ctx + n_dec

    def tile(self, tm):
        tm = min(tm, self.n_ctx, self.dec_seq)
        assert self.n_ctx % tm == 0 and self.dec_seq % tm == 0, (tm, self.n_ctx, self.dec_seq)
        return tm

    def mod_row(self, i, tm):
        r0 = i * tm
        return jnp.where(r0 < self.n_ctx, 0, 1 + (r0 - self.n_ctx) // self.dec_seq)


def _mod_spec(tok, tm, layer, chunk, tn, col_axis):
    def index(*g):
        col = g[col_axis] if col_axis is not None else 0
        return (layer, chunk, tok.mod_row(g[0], tm), 0, col)
    return pl.BlockSpec((None, None, None, 1, tn), index)


def _mod_kernel(c_ref, w_ref, b_ref, o_ref):
    c = c_ref[...]
    s = (c * jax.nn.sigmoid(c)).astype(BF16)
    w = w_ref[...].astype(BF16)
    o_ref[...] = jnp.dot(s, w, preferred_element_type=F32) + b_ref[...]


def _modulation(cvec, w_mod, b_mod, tn=1024):
    depth, d, n = w_mod.shape
    tn = min(tn, n)
    return pl.pallas_call(
        _mod_kernel,
        grid=(depth, n // tn),
        in_specs=[
            pl.BlockSpec((MOD_ROWS, d), lambda l, j: (0, 0)),
            pl.BlockSpec((None, d, tn), lambda l, j: (l, 0, j)),
            pl.BlockSpec((None, 1, tn), lambda l, j: (l, 0, j)),
        ],
        out_specs=pl.BlockSpec((None, MOD_ROWS, tn), lambda l, j: (l, 0, j)),
        out_shape=jax.ShapeDtypeStruct((depth, MOD_ROWS, n), F32),
        compiler_params=_params(("parallel", "parallel")),
        name="modulation",
    )(cvec, w_mod, b_mod.reshape(depth, 1, n))


def _rms(x):
    return x * lax.rsqrt(jnp.mean(x * x, axis=-1, keepdims=True) + RMS_EPS)


def _normmod_kernel(x_ref, nw_ref, sc_ref, sh_ref, h_ref):
    y = _rms(x_ref[...]) * nw_ref[...]
    h_ref[...] = (y * (1.0 + sc_ref[...]) + sh_ref[...]).astype(h_ref.dtype)


def _normmod(tok, x, nw, modv, layer, sc_chunk, sh_chunk, tm=512):
    m, d = x.shape
    tm = tok.tile(tm)
    return pl.pallas_call(
        _normmod_kernel,
        grid=(m // tm,),
        in_specs=[
            pl.BlockSpec((tm, d), lambda i: (i, 0)),
            pl.BlockSpec((1, d), lambda i: (0, 0)),
            _mod_spec(tok, tm, layer, sc_chunk, d, None),
            _mod_spec(tok, tm, layer, sh_chunk, d, None),
        ],
        out_specs=pl.BlockSpec((tm, d), lambda i: (i, 0)),
        out_shape=jax.ShapeDtypeStruct((m, d), BF16),
        compiler_params=_params(("parallel",)),
        name="normmod",
    )(x, nw.reshape(1, d), modv, modv)


def _final_norm_kernel(x_ref, w_ref, o_ref):
    o_ref[...] = _rms(x_ref[...]) * w_ref[...]


def _final_norm(x, w, row0, rows, tm=512):
    _, d = x.shape
    tm = min(tm, rows)
    b0 = row0 // tm
    return pl.pallas_call(
        _final_norm_kernel,
        grid=(rows // tm,),
        in_specs=[
            pl.BlockSpec((tm, d), lambda i: (b0 + i, 0)),
            pl.BlockSpec((1, d), lambda i: (0, 0)),
        ],
        out_specs=pl.BlockSpec((tm, d), lambda i: (i, 0)),
        out_shape=jax.ShapeDtypeStruct((rows, d), F32),
        compiler_params=_params(("parallel",)),
        name="final_norm",
    )(x, w.reshape(1, d))


def _proj_kernel(a_ref, w_ref, o_ref):
    o_ref[...] = jnp.dot(a_ref[...], w_ref[...], preferred_element_type=F32).astype(o_ref.dtype)


def _proj(tok, a, w, tm=1024, tn=1024):
    m, k = a.shape
    n = w.shape[1]
    tm, tn = tok.tile(tm), min(tn, n)
    return pl.pallas_call(
        _proj_kernel,
        grid=(m // tm, n // tn),
        in_specs=[
            pl.BlockSpec((tm, k), lambda i, j: (i, 0)),
            pl.BlockSpec((k, tn), lambda i, j: (0, j)),
        ],
        out_specs=pl.BlockSpec((tm, tn), lambda i, j: (i, j)),
        out_shape=jax.ShapeDtypeStruct((m, n), BF16),
        compiler_params=_params(("parallel", "parallel")),
        name="proj",
    )(a, w)


def _mm_res_kernel(a_ref, w_ref, b_ref, g_ref, x_ref, o_ref, acc_ref, *, nk):
    k = pl.program_id(2)
    p = jnp.dot(a_ref[...], w_ref[...], preferred_element_type=F32)

    def finish(acc):
        o_ref[...] = x_ref[...] + g_ref[...] * (acc + b_ref[...])

    if nk == 1:
        finish(p)
        return

    @pl.when(k == 0)
    def _():
        acc_ref[...] = p

    @pl.when(jnp.logical_and(k > 0, k < nk - 1))
    def _():
        acc_ref[...] += p

    @pl.when(k == nk - 1)
    def _():
        finish(acc_ref[...] + p)


def _mm_res(tok, a, w, b, modv, layer, g_chunk, x, tm=512, tn=1024, tk=4096):
    m, kdim = a.shape
    n = w.shape[1]
    tm, tn, tk = tok.tile(tm), min(tn, n), min(tk, kdim)
    nk = kdim // tk
    return pl.pallas_call(
        functools.partial(_mm_res_kernel, nk=nk),
        grid=(m // tm, n // tn, nk),
        in_specs=[
            pl.BlockSpec((tm, tk), lambda i, j, k: (i, k)),
            pl.BlockSpec((tk, tn), lambda i, j, k: (k, j)),
            pl.BlockSpec((1, tn), lambda i, j, k: (0, j)),
            _mod_spec(tok, tm, layer, g_chunk, tn, 1),
            pl.BlockSpec((tm, tn), lambda i, j, k: (i, j)),
        ],
        out_specs=pl.BlockSpec((tm, tn), lambda i, j, k: (i, j)),
        out_shape=jax.ShapeDtypeStruct((m, n), F32),
        scratch_shapes=[pltpu.VMEM((tm, tn), F32)],
        compiler_params=_params(("parallel", "parallel", "arbitrary")),
        name="mm_res",
    )(a, w, b.reshape(1, n), modv, x)


def _ffn_kernel(h_ref, wa_ref, wv_ref, ca_ref, cv_ref, ba_ref, bv_ref, wd_ref, g_ref, x_ref,
                o_ref, *, tm, nf, n_ctx, seq):
    i, j = pl.program_id(0), pl.program_id(1)
    h = h_ref[...]
    period = jnp.where(i * tm < n_ctx, seq, GRID_W)
    t = lax.broadcasted_iota(jnp.int32, (tm, 1), 0) & (period - 1)
    has_prev = t != 0
    has_next = t != period - 1

    def conv(w_ref, c_ref, b_ref):
        u = jnp.dot(h, w_ref[...], preferred_element_type=F32)
        up = jnp.where(has_prev, pltpu.roll(u, 1, 0), 0.0)
        un = jnp.where(has_next, pltpu.roll(u, tm - 1, 0), 0.0)
        return c_ref[0:1, :] * up + c_ref[1:2, :] * u + c_ref[2:3, :] * un + b_ref[...]

    a = conv(wa_ref, ca_ref, ba_ref)
    v = conv(wv_ref, cv_ref, bv_ref)
    act = ((a * jax.nn.sigmoid(a)) * v).astype(BF16)
    p = jnp.dot(act, wd_ref[...], preferred_element_type=F32)

    @pl.when(j == 0)
    def _():
        o_ref[...] = p

    @pl.when(jnp.logical_and(j > 0, j < nf - 1))
    def _():
        o_ref[...] += p

    @pl.when(j == nf - 1)
    def _():
        o_ref[...] = x_ref[...] + g_ref[...] * (o_ref[...] + p)


def _ffn(tok, h, x, w_up, w_conv, b_conv, w_down, layer, modv, g_chunk, tm=512, tf=256):
    m, d = h.shape
    d_ff = w_down.shape[1]
    tm, tf = tok.tile(tm), min(tf, d_ff)
    assert tm % tok.seq == 0 and tm % GRID_W == 0
    nf = d_ff // tf
    b3 = b_conv.reshape(b_conv.shape[0], 1, 2 * d_ff)
    return pl.pallas_call(
        functools.partial(_ffn_kernel, tm=tm, nf=nf, n_ctx=tok.n_ctx, seq=tok.seq),
        grid=(m // tm, nf),
        in_specs=[
            pl.BlockSpec((tm, d), lambda i, j: (i, 0)),
            pl.BlockSpec((None, d, tf), lambda i, j: (layer, 0, j)),
            pl.BlockSpec((None, d, tf), lambda i, j: (layer, 0, nf + j)),
            pl.BlockSpec((None, 3, tf), lambda i, j: (layer, 0, j)),
            pl.BlockSpec((None, 3, tf), lambda i, j: (layer, 0, nf + j)),
            pl.BlockSpec((None, 1, tf), lambda i, j: (layer, 0, j)),
            pl.BlockSpec((None, 1, tf), lambda i, j: (layer, 0, nf + j)),
            pl.BlockSpec((None, tf, d), lambda i, j: (layer, j, 0)),
            _mod_spec(tok, tm, layer, g_chunk, d, None),
            pl.BlockSpec((tm, d), lambda i, j: (i, 0), pipeline_mode=pl.Buffered(1)),
        ],
        out_specs=pl.BlockSpec((tm, d), lambda i, j: (i, 0)),
        out_shape=jax.ShapeDtypeStruct((m, d), F32),
        compiler_params=_params(("parallel", "arbitrary")),
        name="conv_ffn",
    )(h, w_up, w_up, w_conv, w_conv, b3, b3, w_down, modv, x)


def _dft_tables(n, scale):
    j = jnp.arange(n, dtype=jnp.int32)
    ang = ((j[:, None] * j[None, :]) % n).astype(F32) * (2.0 * np.pi / n)
    return jnp.cos(ang) * scale, jnp.sin(ang) * scale


def _chan_dft_kernel(h_ref, cs_ref, a_ref):
    a_ref[...] = jnp.dot(h_ref[...], cs_ref[...], preferred_element_type=F32).astype(a_ref.dtype)


def _chan_dft(tok, h, cs, tm=1024):
    m, d = h.shape
    cg = d // FOURIER_GROUPS
    tm = tok.tile(tm)
    return pl.pallas_call(
        _chan_dft_kernel,
        grid=(m // tm, FOURIER_GROUPS),
        in_specs=[
            pl.BlockSpec((tm, cg), lambda i, g: (i, g)),
            pl.BlockSpec((cg, 2 * cg), lambda i, g: (0, 0)),
        ],
        out_specs=pl.BlockSpec((tm, 2 * cg), lambda i, g: (i, g)),
        out_shape=jax.ShapeDtypeStruct((m, 2 * d), BF16),
        compiler_params=_params(("parallel", "parallel")),
        name="chan_dft",
    )(h, cs)


def _seq_dft_kernel(ct_ref, st_ref, ac_ref, as_ref, f_ref):
    f = jnp.dot(ct_ref[...], ac_ref[...], preferred_element_type=F32)
    f = f - jnp.dot(st_ref[...], as_ref[...], preferred_element_type=F32)
    f_ref[...] = f.astype(f_ref.dtype)


def _seq_dft(a, ct, st, row0, batch, t, d):
    cg = d // FOURIER_GROUPS
    b0 = row0 // t
    return pl.pallas_call(
        _seq_dft_kernel,
        grid=(batch, FOURIER_GROUPS),
        in_specs=[
            pl.BlockSpec((t, t), lambda b, g: (0, 0)),
            pl.BlockSpec((t, t), lambda b, g: (0, 0)),
            pl.BlockSpec((t, cg), lambda b, g: (b0 + b, 2 * g)),
            pl.BlockSpec((t, cg), lambda b, g: (b0 + b, 2 * g + 1)),
        ],
        out_specs=pl.BlockSpec((t, cg), lambda b, g: (b, g)),
        out_shape=jax.ShapeDtypeStruct((batch * t, d), BF16),
        compiler_params=_params(("parallel", "parallel")),
        name="seq_dft",
    )(ct, st, a, a)


def _swap_pairs(x):
    n = x.shape[-1]
    lane = lax.broadcasted_iota(jnp.int32, x.shape, x.ndim - 1)
    return jnp.where((lane & 1) == 0, pltpu.roll(x, n - 1, x.ndim - 1), pltpu.roll(x, 1, x.ndim - 1))


def _scan_kernel(*refs, t, dk, rope, has_s0, emit_state):
    it = iter(refs)
    lg_ref, q_ref, k_ref, v_ref, g_ref, gnw_ref = (next(it) for _ in range(6))
    cos_ref, sin_ref = (next(it), next(it)) if rope else (None, None)
    s0_ref = next(it) if has_s0 else None
    o_ref = next(it)
    st_ref = next(it) if emit_state else None
    qf_ref, kf_ref, osum_ref, s_ref = (next(it) for _ in range(4))

    head = pl.program_id(1)
    c = CHUNK
    n = t // c

    q = q_ref[...].astype(F32)
    k = k_ref[...].astype(F32) * (dk ** -0.5)
    if rope:
        cos, sin = cos_ref[...], sin_ref[...]
        q = q * cos + _swap_pairs(q) * sin
        k = k * cos + _swap_pairs(k) * sin
    qf_ref[...] = q
    kf_ref[...] = k

    ri = lax.broadcasted_iota(jnp.int32, (c, c), 0)
    ci = lax.broadcasted_iota(jnp.int32, (c, c), 1)
    pos = lax.broadcasted_iota(jnp.int32, (c, 1), 0).astype(F32)

    for d in range(2):
        lg = lg_ref[d, head]
        if d == 0:
            rel, q_pow, k_pow = ri - ci, pos + 1.0, (c - 1.0) - pos
        else:
            rel, q_pow, k_pow = ci - ri, c - pos, pos
        intra = jnp.where(rel >= 0, jnp.exp(lg * jnp.maximum(rel, 0).astype(F32)), 0.0)
        q_decay = jnp.exp(lg * q_pow)
        k_decay = jnp.exp(lg * k_pow)
        chunk_decay = jnp.exp(lg * jnp.full((1, 1), c, F32))
        if has_s0:
            s_ref[...] = s0_ref[d]
        else:
            s_ref[...] = jnp.zeros_like(s_ref)
        for cidx in (range(n) if d == 0 else reversed(range(n))):
            rows = pl.ds(cidx * c, c)
            qc, kc, vc = qf_ref[rows, :], kf_ref[rows, :], v_ref[rows, :]
            scores = lax.dot_general(qc.astype(BF16), kc.astype(BF16), (((1,), (1,)), ((), ())),
                                     preferred_element_type=F32) * intra
            s = s_ref[...]
            o = jnp.dot(scores.astype(BF16), vc, preferred_element_type=F32)
            o = o + jnp.dot((qc * q_decay).astype(BF16), s.astype(BF16), preferred_element_type=F32)
            if d == 0:
                osum_ref[rows, :] = o
            else:
                osum_ref[rows, :] += o
            kv = lax.dot_general((kc * k_decay).astype(BF16), vc, (((0,), (0,)), ((), ())),
                                 preferred_element_type=F32)
            s_ref[...] = s * chunk_decay + kv
        if emit_state:
            st_ref[d] = s_ref[...]

    o = osum_ref[...]
    mu = jnp.mean(o, axis=-1, keepdims=True)
    var = jnp.mean(jnp.square(o - mu), axis=-1, keepdims=True)
    o = (o - mu) * lax.rsqrt(var + GN_EPS) * gnw_ref[...]
    g = g_ref[...].astype(F32)
    o_ref[...] = (o * (g * jax.nn.sigmoid(g))).astype(o_ref.dtype)


def _retention_scan(proj, log_g, gn_w, row0, batch, t, dk, dv, rope=None, s0=None, layer_j=0,
                    emit_state=False):
    hh = RET_HEADS
    b0 = row0 // t
    in_specs = [
        pl.BlockSpec(memory_space=pltpu.SMEM),
        pl.BlockSpec((t, dk), lambda b, h: (b0 + b, h)),
        pl.BlockSpec((t, dk), lambda b, h: (b0 + b, hh + h)),
        pl.BlockSpec((t, dv), lambda b, h: (b0 + b, (2 * hh * dk) // dv + h)),
        pl.BlockSpec((t, dv), lambda b, h: (b0 + b, (2 * hh * dk) // dv + hh + h)),
        pl.BlockSpec((1, dv), lambda b, h: (0, h)),
    ]
    args = [log_g, proj, proj, proj, proj, gn_w.reshape(1, hh * dv)]
    if rope is not None:
        in_specs += [pl.BlockSpec((t, dk), lambda b, h: (0, 0))] * 2
        args += list(rope)
    if s0 is not None:
        in_specs.append(pl.BlockSpec((None, None, 2, None, dk, dv), lambda b, h: (b, layer_j, 0, h, 0, 0)))
        args.append(s0)
    out_specs = [pl.BlockSpec((t, dv), lambda b, h: (b, h))]
    out_shape = [jax.ShapeDtypeStruct((batch * t, hh * dv), BF16)]
    if emit_state:
        out_specs.append(pl.BlockSpec((None, 2, None, dk, dv), lambda b, h: (b, 0, h, 0, 0)))
        out_shape.append(jax.ShapeDtypeStruct((batch, 2, hh, dk, dv), F32))
    outs = pl.pallas_call(
        functools.partial(_scan_kernel, t=t, dk=dk, rope=rope is not None, has_s0=s0 is not None,
                          emit_state=emit_state),
        grid=(batch, hh),
        in_specs=in_specs,
        out_specs=out_specs,
        out_shape=out_shape,
        scratch_shapes=[pltpu.VMEM((t, dk), F32), pltpu.VMEM((t, dk), F32),
                        pltpu.VMEM((t, dv), F32), pltpu.VMEM((dk, dv), F32)],
        compiler_params=_params(("parallel", "parallel")),
        name="retention_scan",
    )(*args)
    return outs if emit_state else (outs[0], None)


def _rope_tables(length, dk):
    pairs_axis = dk // 4
    tpos = jnp.arange(length)
    row = (tpos // GRID_W).astype(F32)
    col = (tpos % GRID_W).astype(F32)
    inv_freq = ROPE_BASE ** (-(jnp.arange(pairs_axis, dtype=F32) / pairs_axis))
    ang = jnp.concatenate([row[:, None] * inv_freq, col[:, None] * inv_freq], axis=-1)
    cos, sin = jnp.cos(ang), jnp.sin(ang)
    cos2 = jnp.repeat(cos, 2, axis=-1)
    sin2 = jnp.stack([-sin, sin], axis=-1).reshape(length, dk)
    return cos2, sin2


def kernel(x_prompt, x_sample, state_ret, c, c_ctx, w_mod, b_mod, norm_w, final_norm_w, w_fo, b_fo,
           w_in, w_out, ret_decay, ret_gn_w, w_up, w_conv, b_conv, w_down):
    batch, seq, d = x_prompt.shape
    dec_batch, dec_seq, _ = x_sample.shape
    depth = w_mod.shape[0]
    tok = _Tokens(batch * seq, seq, dec_batch * dec_seq, dec_seq)
    dk = d // RET_HEADS
    dv = w_out.shape[1] // RET_HEADS
    cg = d // FOURIER_GROUPS

    x = jnp.concatenate([x_prompt.reshape(tok.n_ctx, d), x_sample.reshape(tok.n_dec, d)], axis=0)

    cvec = jnp.zeros((MOD_ROWS, d), F32).at[0].set(c_ctx).at[1:1 + dec_batch].set(c)
    modv = _modulation(cvec, w_mod, b_mod)
    modv = modv.reshape(depth, MOD_ROWS, MOD_CHUNKS, d).transpose(0, 2, 1, 3)
    modv = modv[:, :, :1 + dec_batch].reshape(depth, MOD_CHUNKS, 1 + dec_batch, 1, d)

    w_fo_b, w_in_b, w_out_b = w_fo.astype(BF16), w_in.astype(BF16), w_out.astype(BF16)
    w_up_b, w_down_b = w_up.astype(BF16), w_down.astype(BF16)

    cc, sc = _dft_tables(cg, cg ** -0.5)
    cs_chan = jnp.concatenate([cc, sc], axis=1).astype(BF16)
    seq_tabs = {t: tuple(a.astype(BF16) for a in _dft_tables(t, t ** -0.5)) for t in (seq, dec_seq)}
    rope = _rope_tables(dec_seq, dk)
    log_g = jax.nn.log_sigmoid(ret_decay.astype(F32))
    zero_bias = jnp.zeros((d,), F32)

    states = []
    for i in range(depth):
        j = i // 2
        h = _normmod(tok, x, norm_w[i, 0], modv, i, 1, 0)
        if i % 2 == 0:
            a = _chan_dft(tok, h, cs_chan)
            f = jnp.concatenate([
                _seq_dft(a, *seq_tabs[seq], 0, batch, seq, d),
                _seq_dft(a, *seq_tabs[dec_seq], tok.n_ctx, dec_batch, dec_seq, d)], axis=0)
            x = _mm_res(tok, f, w_fo_b[j], b_fo[j], modv, i, 2, x)
        else:
            p = _proj(tok, h, w_in_b[j])
            o_ctx, st = _retention_scan(p, log_g[j], ret_gn_w[j], 0, batch, seq, dk, dv, emit_state=True)
            o_dec, _ = _retention_scan(p, log_g[j], ret_gn_w[j], tok.n_ctx, dec_batch, dec_seq, dk, dv,
                                       rope=rope, s0=state_ret, layer_j=j)
            states.append(st)
            o = jnp.concatenate([o_ctx, o_dec], axis=0)
            x = _mm_res(tok, o, w_out_b[j], zero_bias, modv, i, 2, x)
        h = _normmod(tok, x, norm_w[i, 1], modv, i, 4, 3)
        x = _ffn(tok, h, x, w_up_b, w_conv, b_conv, w_down_b, i, modv, 5)

    y_prompt = _final_norm(x, final_norm_w, 0, tok.n_ctx).reshape(batch, seq, d)
    y_sample = _final_norm(x, final_norm_w, tok.n_ctx, tok.n_dec).reshape(dec_batch, dec_seq, d)
    state_new = jnp.stack(states, axis=1).astype(x_prompt.dtype)
    return (y_prompt, y_sample, state_new)
```

```python
import functools

import numpy as np
import jax
import jax.numpy as jnp
from jax import lax
from jax.experimental import pallas as pl
from jax.experimental.pallas import tpu as pltpu

GRID_W = 64
FOURIER_GROUPS = 4
RET_HEADS = 16
SCAN_CHUNK = 256
ROPE_BASE = 10000.0
RMS_EPS = 1e-6
GN_EPS = 1e-5
MOD_CHUNKS = 6
MOD_ROWS = 8
V7X_VMEM_LIMIT = 56 * 1024 * 1024

F32 = jnp.float32
BF16 = jnp.bfloat16
_ANY = pl.BlockSpec(memory_space=pl.ANY)


def _params(sem, vmem=V7X_VMEM_LIMIT):
    return pltpu.CompilerParams(dimension_semantics=sem, vmem_limit_bytes=vmem)


class _Tokens:
    def __init__(self, n_ctx, seq, n_dec, dec_seq):
        self.n_ctx, self.seq, self.n_dec, self.dec_seq = n_ctx, seq, n_dec, dec_seq
        self.m = n_ctx + n_dec

    def tile(self, tm):
        tm = min(tm, self.n_ctx, self.dec_seq)
        assert self.n_ctx % tm == 0 and self.dec_seq % tm == 0, (tm, self.n_ctx, self.dec_seq)
        return tm

    def mod_row(self, i, tm):
        r0 = i * tm
        return jnp.where(r0 < self.n_ctx, 0, 1 + (r0 - self.n_ctx) // self.dec_seq)


def _mod_spec(tok, tm, layer, chunk, tn, row_axis, col_axis):
    def index(*g):
        col = g[col_axis] if col_axis is not None else 0
        return (layer, chunk, tok.mod_row(g[row_axis], tm), 0, col)
    return pl.BlockSpec((None, None, None, 1, tn), index)


def _mod_kernel(c_ref, w_ref, b_ref, o_ref):
    c = c_ref[...]
    s = (c * jax.nn.sigmoid(c)).astype(BF16)
    w = w_ref[...].astype(BF16)
    o_ref[...] = jnp.dot(s, w, preferred_element_type=F32) + b_ref[...]


def _modulation(cvec, w_mod, b_mod, tn=1024):
    depth, d, n = w_mod.shape
    tn = min(tn, n)
    return pl.pallas_call(
        _mod_kernel,
        grid=(depth, n // tn),
        in_specs=[
            pl.BlockSpec((MOD_ROWS, d), lambda l, j: (0, 0)),
            pl.BlockSpec((None, d, tn), lambda l, j: (l, 0, j)),
            pl.BlockSpec((None, 1, tn), lambda l, j: (l, 0, j)),
        ],
        out_specs=pl.BlockSpec((None, MOD_ROWS, tn), lambda l, j: (l, 0, j)),
        out_shape=jax.ShapeDtypeStruct((depth, MOD_ROWS, n), F32),
        compiler_params=_params(("parallel", "parallel")),
        name="modulation",
    )(cvec, w_mod, b_mod.reshape(depth, 1, n))


def _rms(x):
    return x * lax.rsqrt(jnp.mean(x * x, axis=-1, keepdims=True) + RMS_EPS)


def _normmod_kernel(x_ref, nw_ref, sc_ref, sh_ref, h_ref):
    y = _rms(x_ref[...]) * nw_ref[...]
    h_ref[...] = (y * (1.0 + sc_ref[...]) + sh_ref[...]).astype(h_ref.dtype)


def _normmod(tok, x, nw, modv, layer, sc_chunk, sh_chunk, tm=512):
    m, d = x.shape
    tm = tok.tile(tm)
    return pl.pallas_call(
        _normmod_kernel,
        grid=(m // tm,),
        in_specs=[
            pl.BlockSpec((tm, d), lambda i: (i, 0)),
            pl.BlockSpec((1, d), lambda i: (0, 0)),
            _mod_spec(tok, tm, layer, sc_chunk, d, 0, None),
            _mod_spec(tok, tm, layer, sh_chunk, d, 0, None),
        ],
        out_specs=pl.BlockSpec((tm, d), lambda i: (i, 0)),
        out_shape=jax.ShapeDtypeStruct((m, d), BF16),
        compiler_params=_params(("parallel",)),
        name="normmod",
    )(x, nw.reshape(1, d), modv, modv)


def _final_norm_kernel(x_ref, w_ref, o_ref):
    o_ref[...] = _rms(x_ref[...]) * w_ref[...]


def _final_norm(x, w, row0, rows, tm=512):
    _, d = x.shape
    tm = min(tm, rows)
    b0 = row0 // tm
    return pl.pallas_call(
        _final_norm_kernel,
        grid=(rows // tm,),
        in_specs=[
            pl.BlockSpec((tm, d), lambda i: (b0 + i, 0)),
            pl.BlockSpec((1, d), lambda i: (0, 0)),
        ],
        out_specs=pl.BlockSpec((tm, d), lambda i: (i, 0)),
        out_shape=jax.ShapeDtypeStruct((rows, d), F32),
        compiler_params=_params(("parallel",)),
        name="final_norm",
    )(x, w.reshape(1, d))


def _proj_kernel(a_ref, w_ref, o_ref, wb_ref):
    @pl.when(pl.program_id(1) == 0)
    def _():
        wb_ref[...] = w_ref[...].astype(BF16)

    o_ref[...] = jnp.dot(a_ref[...], wb_ref[...], preferred_element_type=F32).astype(o_ref.dtype)


def _proj(tok, a, w, layer, tm=1024, tn=512):
    m, k = a.shape
    n = w.shape[2]
    tm, tn = tok.tile(tm), min(tn, n)
    return pl.pallas_call(
        _proj_kernel,
        grid=(n // tn, m // tm),
        in_specs=[
            pl.BlockSpec((tm, k), lambda j, i: (i, 0)),
            pl.BlockSpec((None, k, tn), lambda j, i: (layer, 0, j)),
        ],
        out_specs=pl.BlockSpec((tm, tn), lambda j, i: (i, j)),
        out_shape=jax.ShapeDtypeStruct((m, n), BF16),
        scratch_shapes=[pltpu.VMEM((k, tn), BF16)],
        compiler_params=_params(("parallel", "arbitrary")),
        name="proj",
    )(a, w)


def _mm_res_kernel(a_ref, w_ref, b_ref, g_ref, x_ref, o_ref, acc_ref, *, nk):
    k = pl.program_id(2)
    p = jnp.dot(a_ref[...], w_ref[...], preferred_element_type=F32)

    def finish(acc):
        o_ref[...] = x_ref[...] + g_ref[...] * (acc + b_ref[...])

    if nk == 1:
        finish(p)
        return

    @pl.when(k == 0)
    def _():
        acc_ref[...] = p

    @pl.when(jnp.logical_and(k > 0, k < nk - 1))
    def _():
        acc_ref[...] += p

    @pl.when(k == nk - 1)
    def _():
        finish(acc_ref[...] + p)


def _mm_res(tok, a, w, layer_j, b, modv, layer, g_chunk, x, tm=512, tn=1024, tk=4096):
    m, kdim = a.shape
    n = w.shape[2]
    tm, tn, tk = tok.tile(tm), min(tn, n), min(tk, kdim)
    nk = kdim // tk
    return pl.pallas_call(
        functools.partial(_mm_res_kernel, nk=nk),
        grid=(m // tm, n // tn, nk),
        in_specs=[
            pl.BlockSpec((tm, tk), lambda i, j, k: (i, k)),
            pl.BlockSpec((None, tk, tn), lambda i, j, k: (layer_j, k, j)),
            pl.BlockSpec((1, tn), lambda i, j, k: (0, j)),
            _mod_spec(tok, tm, layer, g_chunk, tn, 0, 1),
            pl.BlockSpec((tm, tn), lambda i, j, k: (i, j)),
        ],
        out_specs=pl.BlockSpec((tm, tn), lambda i, j, k: (i, j)),
        out_shape=jax.ShapeDtypeStruct((m, n), F32),
        scratch_shapes=[pltpu.VMEM((tm, tn), F32)],
        compiler_params=_params(("parallel", "parallel", "arbitrary")),
        name="mm_res",
    )(a, w, b.reshape(1, n), modv, x)


def _ffn_kernel(h_ref, wa_ref, wv_ref, ca_ref, cv_ref, ba_ref, bv_ref, wd_ref, g_ref, x_ref,
                o_ref, ua_ref, uv_ref, *, tm, nf, n_ctx, seq):
    i, j = pl.program_id(0), pl.program_id(1)

    @pl.when(j == 0)
    def _():
        ua_ref[...] = jnp.zeros_like(ua_ref)
        uv_ref[...] = jnp.zeros_like(uv_ref)
        o_ref[...] = jnp.zeros_like(o_ref)

    period = jnp.where(i * tm < n_ctx, seq, GRID_W)
    t = lax.broadcasted_iota(jnp.int32, (tm, 1), 0) & (period - 1)
    has_prev = t != 0
    has_next = t != period - 1

    def conv(u, c_ref, b_ref):
        up = jnp.where(has_prev, pltpu.roll(u, 1, 0), 0.0)
        un = jnp.where(has_next, pltpu.roll(u, tm - 1, 0), 0.0)
        return c_ref[0:1, :] * up + c_ref[1:2, :] * u + c_ref[2:3, :] * un + b_ref[...]

    a = conv(ua_ref[...], ca_ref, ba_ref)
    v = conv(uv_ref[...], cv_ref, bv_ref)
    act = jnp.where(j > 0, (a * jax.nn.sigmoid(a)) * v, 0.0).astype(BF16)
    o_ref[...] += jnp.dot(act, wd_ref[...], preferred_element_type=F32)

    h = h_ref[...]
    ua_ref[...] = jnp.dot(h, wa_ref[...], preferred_element_type=F32)
    uv_ref[...] = jnp.dot(h, wv_ref[...], preferred_element_type=F32)

    @pl.when(j == nf)
    def _():
        o_ref[...] = x_ref[...] + g_ref[...] * o_ref[...]


def _ffn(tok, h, x, w_up, w_conv, b_conv, w_down, layer, modv, g_chunk, tm=512, tf=256):
    m, d = h.shape
    d_ff = w_down.shape[1]
    tm, tf = tok.tile(tm), min(tf, d_ff)
    assert tm % tok.seq == 0 and tm % GRID_W == 0
    nf = d_ff // tf
    b3 = b_conv.reshape(b_conv.shape[0], 1, 2 * d_ff)

    def up(j):
        return jnp.minimum(j, nf - 1)

    def down(j):
        return jnp.maximum(j - 1, 0)

    return pl.pallas_call(
        functools.partial(_ffn_kernel, tm=tm, nf=nf, n_ctx=tok.n_ctx, seq=tok.seq),
        grid=(m // tm, nf + 1),
        in_specs=[
            pl.BlockSpec((tm, d), lambda i, j: (i, 0)),
            pl.BlockSpec((None, d, tf), lambda i, j: (layer, 0, up(j))),
            pl.BlockSpec((None, d, tf), lambda i, j: (layer, 0, nf + up(j))),
            pl.BlockSpec((None, 3, tf), lambda i, j: (layer, 0, down(j))),
            pl.BlockSpec((None, 3, tf), lambda i, j: (layer, 0, nf + down(j))),
            pl.BlockSpec((None, 1, tf), lambda i, j: (layer, 0, down(j))),
            pl.BlockSpec((None, 1, tf), lambda i, j: (layer, 0, nf + down(j))),
            pl.BlockSpec((None, tf, d), lambda i, j: (layer, down(j), 0)),
            _mod_spec(tok, tm, layer, g_chunk, d, 0, None),
            pl.BlockSpec((tm, d), lambda i, j: (i, 0), pipeline_mode=pl.Buffered(1)),
        ],
        out_specs=pl.BlockSpec((tm, d), lambda i, j: (i, 0)),
        out_shape=jax.ShapeDtypeStruct((m, d), F32),
        scratch_shapes=[pltpu.VMEM((tm, tf), F32), pltpu.VMEM((tm, tf), F32)],
        compiler_params=_params(("parallel", "arbitrary")),
        name="conv_ffn",
    )(h, w_up, w_up, w_conv, w_conv, b3, b3, w_down, modv, x)


def _dft_tables(n, scale):
    j = jnp.arange(n, dtype=jnp.int32)
    ang = ((j[:, None] * j[None, :]) % n).astype(F32) * (2.0 * np.pi / n)
    return jnp.cos(ang) * scale, jnp.sin(ang) * scale


def _chan_dft_kernel(h_ref, cs_ref, a_ref):
    a_ref[...] = jnp.dot(h_ref[...], cs_ref[...], preferred_element_type=F32).astype(a_ref.dtype)


def _chan_dft(tok, h, cs, tm=1024):
    m, d = h.shape
    cg = d // FOURIER_GROUPS
    tm = tok.tile(tm)
    return pl.pallas_call(
        _chan_dft_kernel,
        grid=(m // tm, FOURIER_GROUPS),
        in_specs=[
            pl.BlockSpec((tm, cg), lambda i, g: (i, g)),
            pl.BlockSpec((cg, 2 * cg), lambda i, g: (0, 0)),
        ],
        out_specs=pl.BlockSpec((tm, 2 * cg), lambda i, g: (i, g)),
        out_shape=jax.ShapeDtypeStruct((m, 2 * d), BF16),
        compiler_params=_params(("parallel", "parallel")),
        name="chan_dft",
    )(h, cs)


def _seq_dft_kernel(ct_ref, st_ref, ac_ref, as_ref, *rest):
    f_ref = rest[-1]
    f = jnp.dot(ct_ref[...], ac_ref[...], preferred_element_type=F32)
    f = f - jnp.dot(st_ref[...], as_ref[...], preferred_element_type=F32)
    f_ref[...] = f.astype(f_ref.dtype)


def _seq_dft(a, ct, st, row0, batch, t, d, into=None):
    cg = d // FOURIER_GROUPS
    b0 = row0 // t
    in_specs = [
        pl.BlockSpec((t, t), lambda b, g: (0, 0)),
        pl.BlockSpec((t, t), lambda b, g: (0, 0)),
        pl.BlockSpec((t, cg), lambda b, g: (b0 + b, 2 * g)),
        pl.BlockSpec((t, cg), lambda b, g: (b0 + b, 2 * g + 1)),
    ]
    args = [ct, st, a, a]
    aliases = {}
    if into is not None:
        aliases[len(args)] = 0
        in_specs.append(_ANY)
        args.append(into)
    return pl.pallas_call(
        _seq_dft_kernel,
        grid=(batch, FOURIER_GROUPS),
        in_specs=in_specs,
        out_specs=pl.BlockSpec((t, cg), lambda b, g: (b0 + b, g)),
        out_shape=jax.ShapeDtypeStruct((a.shape[0], d), BF16),
        input_output_aliases=aliases,
        compiler_params=_params(("parallel", "parallel")),
        name="seq_dft",
    )(*args)


def _swap_pairs(x):
    n = x.shape[-1]
    lane = lax.broadcasted_iota(jnp.int32, x.shape, x.ndim - 1)
    return jnp.where((lane & 1) == 0, pltpu.roll(x, n - 1, x.ndim - 1), pltpu.roll(x, 1, x.ndim - 1))


def _scan_kernel(*refs, t, c, dk, dv, hp, rope, has_s0, emit_state, n_alias):
    it = iter(refs)
    lg_ref, q_ref, k_ref, v_ref, g_ref, gnw_ref = (next(it) for _ in range(6))
    cos_ref, sin_ref = (next(it), next(it)) if rope else (None, None)
    s0_ref = next(it) if has_s0 else None
    for _ in range(n_alias):
        next(it)
    o_ref = next(it)
    st_ref = next(it) if emit_state else None
    qf_ref, kf_ref, osum_ref, s_ref = (next(it) for _ in range(4))

    hg = pl.program_id(1)
    n = t // c

    q = q_ref[...].astype(F32)
    k = k_ref[...].astype(F32) * (dk ** -0.5)
    if rope:
        cos = jnp.concatenate([cos_ref[...]] * hp, axis=1)
        sin = jnp.concatenate([sin_ref[...]] * hp, axis=1)
        q = q * cos + _swap_pairs(q) * sin
        k = k * cos + _swap_pairs(k) * sin
    qf_ref[...] = q
    kf_ref[...] = k

    ri = lax.broadcasted_iota(jnp.int32, (c, c), 0)
    ci = lax.broadcasted_iota(jnp.int32, (c, c), 1)
    pos = lax.broadcasted_iota(jnp.int32, (c, 1), 0).astype(F32)

    for hh in range(hp):
        kcols = pl.ds(hh * dk, dk)
        vcols = pl.ds(hh * dv, dv)
        for d in range(2):
            lg = lg_ref[d, hg * hp + hh]
            if d == 0:
                rel, q_pow, k_pow = ri - ci, pos + 1.0, (c - 1.0) - pos
            else:
                rel, q_pow, k_pow = ci - ri, c - pos, pos
            intra = jnp.where(rel >= 0, jnp.exp(lg * jnp.maximum(rel, 0).astype(F32)), 0.0)
            q_decay = jnp.exp(lg * q_pow)
            k_decay = jnp.exp(lg * k_pow)
            chunk_decay = jnp.exp(lg * jnp.full((1, 1), c, F32))
            order = list(range(n)) if d == 0 else list(reversed(range(n)))
            for step, cidx in enumerate(order):
                rows = pl.ds(cidx * c, c)
                qc, kc, vc = qf_ref[rows, kcols], kf_ref[rows, kcols], v_ref[rows, vcols]
                scores = lax.dot_general(qc.astype(BF16), kc.astype(BF16), (((1,), (1,)), ((), ())),
                                         preferred_element_type=F32) * intra
                o = jnp.dot(scores.astype(BF16), vc, preferred_element_type=F32)
                if step > 0:
                    s = s_ref[hh, d]
                elif has_s0:
                    s = s0_ref[d, hh]
                else:
                    s = None
                if s is not None:
                    o = o + jnp.dot((qc * q_decay).astype(BF16), s.astype(BF16),
                                    preferred_element_type=F32)
                if d == 0:
                    osum_ref[rows, vcols] = o
                else:
                    osum_ref[rows, vcols] += o
                last = step == n - 1
                if last and not emit_state:
                    continue
                kv = lax.dot_general((kc * k_decay).astype(BF16), vc, (((0,), (0,)), ((), ())),
                                     preferred_element_type=F32)
                s_new = kv if s is None else s * chunk_decay + kv
                if last:
                    st_ref[d, hh] = s_new
                else:
                    s_ref[hh, d] = s_new

    for hh in range(hp):
        vcols = pl.ds(hh * dv, dv)
        o = osum_ref[:, vcols]
        mu = jnp.mean(o, axis=-1, keepdims=True)
        var = jnp.mean(jnp.square(o - mu), axis=-1, keepdims=True)
        o = (o - mu) * lax.rsqrt(var + GN_EPS) * gnw_ref[:, vcols]
        g = g_ref[:, vcols].astype(F32)
        o_ref[:, vcols] = (o * (g * jax.nn.sigmoid(g))).astype(o_ref.dtype)


def _retention_scan(proj, log_g, gn_w, row0, batch, t, dk, dv, hp, layer_j, rope=None, s0=None,
                    o_into=None, state_shape=None, state_into=None):
    hh = RET_HEADS
    assert hh % hp == 0
    c = min(SCAN_CHUNK, t)
    b0 = row0 // t
    vblk = (2 * hh * dk) // (hp * dv)
    emit_state = state_shape is not None
    in_specs = [
        pl.BlockSpec(memory_space=pltpu.SMEM),
        pl.BlockSpec((t, hp * dk), lambda b, h: (b0 + b, h)),
        pl.BlockSpec((t, hp * dk), lambda b, h: (b0 + b, hh // hp + h)),
        pl.BlockSpec((t, hp * dv), lambda b, h: (b0 + b, vblk + h)),
        pl.BlockSpec((t, hp * dv), lambda b, h: (b0 + b, vblk + hh // hp + h)),
        pl.BlockSpec((1, hp * dv), lambda b, h: (0, h)),
    ]
    args = [log_g, proj, proj, proj, proj, gn_w.reshape(1, hh * dv)]
    if rope is not None:
        in_specs += [pl.BlockSpec((t, dk), lambda b, h: (0, 0))] * 2
        args += list(rope)
    if s0 is not None:
        in_specs.append(pl.BlockSpec((None, None, 2, hp, dk, dv), lambda b, h: (b, layer_j, 0, h, 0, 0)))
        args.append(s0)
    aliases = {}
    for out_idx, buf in enumerate((o_into, state_into)):
        if buf is not None:
            aliases[len(args)] = out_idx
            in_specs.append(_ANY)
            args.append(buf)
    out_specs = [pl.BlockSpec((t, hp * dv), lambda b, h: (b0 + b, h))]
    out_shape = [jax.ShapeDtypeStruct((proj.shape[0], hh * dv), BF16)]
    if emit_state:
        out_specs.append(pl.BlockSpec((None, None, 2, hp, dk, dv), lambda b, h: (b, layer_j, 0, h, 0, 0)))
        out_shape.append(jax.ShapeDtypeStruct(state_shape, F32))
    outs = pl.pallas_call(
        functools.partial(_scan_kernel, t=t, c=c, dk=dk, dv=dv, hp=hp, rope=rope is not None,
                          has_s0=s0 is not None, emit_state=emit_state, n_alias=len(aliases)),
        grid=(batch, hh // hp),
        in_specs=in_specs,
        out_specs=out_specs,
        out_shape=out_shape,
        input_output_aliases=aliases,
        scratch_shapes=[pltpu.VMEM((t, hp * dk), F32), pltpu.VMEM((t, hp * dk), F32),
                        pltpu.VMEM((t, hp * dv), F32), pltpu.VMEM((hp, 2, dk, dv), F32)],
        compiler_params=_params(("parallel", "parallel")),
        name="retention_scan",
    )(*args)
    return outs if emit_state else (outs[0], None)


def _rope_tables(length, dk):
    pairs_axis = dk // 4
    tpos = jnp.arange(length)
    row = (tpos // GRID_W).astype(F32)
    col = (tpos % GRID_W).astype(F32)
    inv_freq = ROPE_BASE ** (-(jnp.arange(pairs_axis, dtype=F32) / pairs_axis))
    ang = jnp.concatenate([row[:, None] * inv_freq, col[:, None] * inv_freq], axis=-1)
    cos, sin = jnp.cos(ang), jnp.sin(ang)
    cos2 = jnp.repeat(cos, 2, axis=-1)
    sin2 = jnp.stack([-sin, sin], axis=-1).reshape(length, dk)
    return cos2, sin2


def kernel(x_prompt, x_sample, state_ret, c, c_ctx, w_mod, b_mod, norm_w, final_norm_w, w_fo, b_fo,
           w_in, w_out, ret_decay, ret_gn_w, w_up, w_conv, b_conv, w_down):
    batch, seq, d = x_prompt.shape
    dec_batch, dec_seq, _ = x_sample.shape
    depth = w_mod.shape[0]
    n_ret = w_in.shape[0]
    tok = _Tokens(batch * seq, seq, dec_batch * dec_seq, dec_seq)
    dk = d // RET_HEADS
    dv = w_out.shape[1] // RET_HEADS
    cg = d // FOURIER_GROUPS
    state_shape = (batch, n_ret, 2, RET_HEADS, dk, dv)

    x = jnp.concatenate([x_prompt.reshape(tok.n_ctx, d), x_sample.reshape(tok.n_dec, d)], axis=0)

    cvec = jnp.zeros((MOD_ROWS, d), F32).at[0].set(c_ctx).at[1:1 + dec_batch].set(c)
    modv = _modulation(cvec, w_mod, b_mod)
    modv = modv.reshape(depth, MOD_ROWS, MOD_CHUNKS, d).transpose(0, 2, 1, 3)
    modv = modv[:, :, :1 + dec_batch].reshape(depth, MOD_CHUNKS, 1 + dec_batch, 1, d)

    w_fo_b, w_out_b = w_fo.astype(BF16), w_out.astype(BF16)
    w_up_b, w_down_b = w_up.astype(BF16), w_down.astype(BF16)

    cc, sc = _dft_tables(cg, cg ** -0.5)
    cs_chan = jnp.concatenate([cc, sc], axis=1).astype(BF16)
    seq_tabs = {t: tuple(a.astype(BF16) for a in _dft_tables(t, t ** -0.5)) for t in (seq, dec_seq)}
    rope = _rope_tables(dec_seq, dk)
    log_g = jax.nn.log_sigmoid(ret_decay.astype(F32))
    zero_bias = jnp.zeros((d,), F32)

    state_new = None
    for i in range(depth):
        j = i // 2
        h = _normmod(tok, x, norm_w[i, 0], modv, i, 1, 0)
        if i % 2 == 0:
            a = _chan_dft(tok, h, cs_chan)
            f = _seq_dft(a, *seq_tabs[seq], 0, batch, seq, d)
            f = _seq_dft(a, *seq_tabs[dec_seq], tok.n_ctx, dec_batch, dec_seq, d, into=f)
            x = _mm_res(tok, f, w_fo_b, j, b_fo[j], modv, i, 2, x)
        else:
            p = _proj(tok, h, w_in, j)
            o, state_new = _retention_scan(p, log_g[j], ret_gn_w[j], 0, batch, seq, dk, dv, 4, j,
                                           state_shape=state_shape, state_into=state_new)
            o, _ = _retention_scan(p, log_g[j], ret_gn_w[j], tok.n_ctx, dec_batch, dec_seq, dk, dv, 2, j,
                                   rope=rope, s0=state_ret, o_into=o)
            x = _mm_res(tok, o, w_out_b, j, zero_bias, modv, i, 2, x)
        h = _normmod(tok, x, norm_w[i, 1], modv, i, 4, 3)
        x = _ffn(tok, h, x, w_up_b, w_conv, b_conv, w_down_b, i, modv, 5)

    y_prompt = _final_norm(x, final_norm_w, 0, tok.n_ctx).reshape(batch, seq, d)
    y_sample = _final_norm(x, final_norm_w, tok.n_ctx, tok.n_dec).reshape(dec_batch, dec_seq, d)
    return (y_prompt, y_sample, state_new.astype(x_prompt.dtype))
```
